```python
import math
import jax, jax.numpy as jnp
from jax import lax
import numpy as np

D_MODEL = 2048
BATCH = 16
SEQ = 256
DEPTH = 4
DEC_BATCH = 4
DEC_SEQ = 1024
PAST_LEN = 512

GRID_W = 64
HEAD_DIM = 128
N_MIX_HEADS = D_MODEL // HEAD_DIM
GLA_HEADS = N_MIX_HEADS // 4
HGRN_HEADS = N_MIX_HEADS // 4
NAT_HEADS = N_MIX_HEADS - GLA_HEADS - HGRN_HEADS
GLA_W = GLA_HEADS * HEAD_DIM
HGRN_W = HGRN_HEADS * HEAD_DIM
NAT_W = NAT_HEADS * HEAD_DIM
MIX_W = GLA_W + HGRN_W + NAT_W
GLA_LOWRANK = 16
GLA_TAU = 16.0
CHUNK = 16
NAT_KR = 8
NAT_KC = 16
ROPE_BASE = 10000.0
ROPE_AXIS = HEAD_DIM // 2
N_EXPERTS = 16
EXPERT_FF = D_MODEL // 2
EC_CAPACITY = 2
Q_BLOCK = 128
EPS = 1e-6
N_MOD = 6
IN_SPLITS = (('gla_q', GLA_W), ('gla_k', GLA_W), ('gla_v', GLA_W), ('gla_g', GLA_W),
             ('gla_af', GLA_LOWRANK), ('gla_ab', GLA_LOWRANK),
             ('hg_q', HGRN_W), ('hg_i', HGRN_W), ('hg_g', HGRN_W), ('hg_ff', HGRN_W), ('hg_fb', HGRN_W),
             ('nat_q', NAT_W), ('nat_k', NAT_W), ('nat_v', NAT_W))
P_IN = 4 * GLA_W + 2 * GLA_LOWRANK + 5 * HGRN_W + 3 * NAT_W

kernel_name = 'hybrid_gla_hgrn2_natten_ec_moe_diffusion_step'


def rmsnorm(x, g):
    xf = x.astype(jnp.float32)
    y = xf * lax.rsqrt(jnp.mean(xf * xf, axis=-1, keepdims=True) + EPS)
    return (y * g.astype(jnp.float32)).astype(x.dtype)


def to_heads(x, h):
    b, t, _ = x.shape
    return x.reshape(b, t, h, HEAD_DIM).transpose(0, 2, 1, 3)


def from_heads(x):
    b, h, t, d = x.shape
    return x.transpose(0, 2, 1, 3).reshape(b, t, h * d)


def split_cols(p):
    out, o = {}, 0
    for name, w in IN_SPLITS:
        out[name] = p[..., o:o + w]
        o += w
    return out


def axial_rope_tables(t, dtype):
    pos = jnp.arange(t)
    rows = (pos // GRID_W).astype(jnp.float32)
    cols = (pos % GRID_W).astype(jnp.float32)
    inv = ROPE_BASE ** (-jnp.arange(0, ROPE_AXIS, 2, dtype=jnp.float32) / ROPE_AXIS)
    ar = rows[:, None] * inv[None, :]
    ac = cols[:, None] * inv[None, :]
    return (jnp.cos(ar).astype(dtype), jnp.sin(ar).astype(dtype),
            jnp.cos(ac).astype(dtype), jnp.sin(ac).astype(dtype))


def rotate(x, cos, sin):
    x1, x2 = jnp.split(x, 2, axis=-1)
    return jnp.concatenate([x1 * cos - x2 * sin, x1 * sin + x2 * cos], axis=-1)


def apply_axial_rope(x, tabs):
    cos_r, sin_r, cos_c, sin_c = tabs
    return jnp.concatenate([rotate(x[..., :ROPE_AXIS], cos_r, sin_r),
                            rotate(x[..., ROPE_AXIS:], cos_c, sin_c)], axis=-1)


def gated_scan(q, k, v, log_a, s0):
    b, h, t, dk = q.shape
    dv = v.shape[-1]
    n = t // CHUNK
    f32 = jnp.float32
    qc = q.astype(f32).reshape(b, h, n, CHUNK, dk)
    kc = k.astype(f32).reshape(b, h, n, CHUNK, dk)
    vc = v.astype(f32).reshape(b, h, n, CHUNK, dv)
    cum = jnp.cumsum(log_a.astype(f32).reshape(b, h, n, CHUNK, dk), axis=3)
    total = cum[:, :, :, -1]
    causal = jnp.tril(jnp.ones((CHUNK, CHUNK), dtype=bool))
    diff = cum[..., :, None, :] - cum[..., None, :, :]
    decay = jnp.exp(jnp.where(causal[:, :, None], diff, -jnp.inf))
    scores = jnp.einsum('bhnjd,bhnid,bhnjid->bhnji', qc, kc, decay)
    o_intra = jnp.einsum('bhnji,bhnie->bhnje', scores, vc)
    q_dec = qc * jnp.exp(cum)
    k_dec = kc * jnp.exp(total[..., None, :] - cum)
    kv_chunk = jnp.einsum('bhnid,bhnie->bhnde', k_dec, vc)

    def step(s, inp):
        kv_n, tot_n, qd_n = inp
        o_n = jnp.einsum('bhjd,bhde->bhje', qd_n, s)
        s = s * jnp.exp(tot_n)[..., None] + kv_n
        return s, o_n

    xs = (jnp.moveaxis(kv_chunk, 2, 0), jnp.moveaxis(total, 2, 0), jnp.moveaxis(q_dec, 2, 0))
    s_final, o_inter = lax.scan(step, s0.astype(f32), xs)
    o = o_intra + jnp.moveaxis(o_inter, 0, 2)
    return o.reshape(b, h, t, dv).astype(v.dtype), s_final.astype(s0.dtype)


def flip_t(a):
    return jnp.flip(a, axis=2)


def hgrn_forget(z, lb):
    zf = z.astype(jnp.float32)
    lbf = lb.astype(jnp.float32).reshape(HGRN_HEADS, 1, HEAD_DIM)
    log_f = jnp.logaddexp(jnp.log(lbf), jnp.log1p(-lbf) + jax.nn.log_sigmoid(zf))
    k = (1.0 - lbf) * jax.nn.sigmoid(-zf)
    return log_f, k


def dense_attn(q, k, v):
    b, h, t, d = q.shape
    nb = t // Q_BLOCK
    qb = q.reshape(b, h, nb, Q_BLOCK, d).transpose(2, 0, 1, 3, 4)

    def blk(qi):
        s = jnp.einsum('bhqd,bhkd->bhqk', qi, k).astype(jnp.float32)
        p = jax.nn.softmax(s, axis=-1).astype(v.dtype)
        return jnp.einsum('bhqk,bhkd->bhqd', p, v)

    o = lax.map(blk, qb)
    return o.transpose(1, 2, 0, 3, 4).reshape(b, h, t, d)


def nat_latent(q, k, v, ck, cv, rpb):
    b, h, t, d = q.shape
    rows = t // GRID_W
    kr = min(NAT_KR, rows)
    qg = q.reshape(b, h, rows, GRID_W, d)
    kg = k.reshape(b, h, rows, GRID_W, d)
    vg = v.reshape(b, h, rows, GRID_W, d)
    cols = jnp.arange(GRID_W)
    c_start = jnp.clip(cols - NAT_KC // 2, 0, GRID_W - NAT_KC)
    col_in = (cols[None, :] >= c_start[:, None]) & (cols[None, :] < c_start[:, None] + NAT_KC)
    col_idx = jnp.clip(cols[None, :] - cols[:, None], -(NAT_KC - 1), NAT_KC - 1) + NAT_KC - 1

    def row_block(r):
        start = jnp.clip(r - kr // 2, 0, rows - kr)
        q_r = lax.dynamic_index_in_dim(qg, r, axis=2, keepdims=False)
        k_r = lax.dynamic_slice_in_dim(kg, start, kr, axis=2).reshape(b, h, kr * GRID_W, d)
        v_r = lax.dynamic_slice_in_dim(vg, start, kr, axis=2).reshape(b, h, kr * GRID_W, d)
        row_idx = start + jnp.arange(kr) - r + NAT_KR - 1
        bias = rpb[:, row_idx[None, :, None], col_idx[:, None, :]].astype(jnp.float32)
        s_lat = jnp.einsum('bhqd,bhkd->bhqk', q_r, k_r).astype(jnp.float32)
        s_lat = s_lat.reshape(b, h, GRID_W, kr, GRID_W) + bias
        s_lat = jnp.where(col_in[:, None, :], s_lat, -jnp.inf).reshape(b, h, GRID_W, kr * GRID_W)
        s_ctx = jnp.einsum('bhqd,bhcd->bhqc', q_r, ck).astype(jnp.float32)
        p = jax.nn.softmax(jnp.concatenate([s_lat, s_ctx], axis=-1), axis=-1).astype(v.dtype)
        nl = kr * GRID_W
        return (jnp.einsum('bhqk,bhkd->bhqd', p[..., :nl], v_r)
                + jnp.einsum('bhqc,bhcd->bhqd', p[..., nl:], cv))

    out = lax.map(row_block, jnp.arange(rows))
    return out.transpose(1, 2, 0, 3, 4).reshape(b, h, t, d)


def mixer(h, lw, lb_f, lb_b, states, rope):
    b, t, _ = h.shape
    latent = states is not None
    p = split_cols(h @ lw['w_in'])
    if latent:
        gla_f0, gla_b0, hg_f0, hg_b0, ctx_k, ctx_v = states
    else:
        gla_f0 = gla_b0 = jnp.zeros((b, GLA_HEADS, HEAD_DIM, HEAD_DIM), h.dtype)
        hg_f0 = hg_b0 = jnp.zeros((b, HGRN_HEADS, HEAD_DIM, HEAD_DIM), h.dtype)

    q = to_heads(p['gla_q'], GLA_HEADS) * (HEAD_DIM ** -0.5)
    k = to_heads(p['gla_k'], GLA_HEADS)
    if latent:
        q = apply_axial_rope(q, rope)
        k = apply_axial_rope(k, rope)
    v = to_heads(p['gla_v'], GLA_HEADS)
    la_f = to_heads(jax.nn.log_sigmoid((p['gla_af'] @ lw['gla_wa_f'] + lw['gla_ba_f']).astype(jnp.float32)) / GLA_TAU, GLA_HEADS)
    la_b = to_heads(jax.nn.log_sigmoid((p['gla_ab'] @ lw['gla_wa_b'] + lw['gla_ba_b']).astype(jnp.float32)) / GLA_TAU, GLA_HEADS)
    o_f, gs_f = gated_scan(q, k, v, la_f, gla_f0)
    o_b, gs_b = gated_scan(flip_t(q), flip_t(k), flip_t(v), flip_t(la_b), gla_b0)
    o_gla = rmsnorm(o_f + flip_t(o_b), lw['gla_onorm']) * jax.nn.silu(to_heads(p['gla_g'], GLA_HEADS))

    q = jax.nn.silu(to_heads(p['hg_q'], HGRN_HEADS))
    iv = to_heads(p['hg_i'], HGRN_HEADS)
    lf_f, k_f = hgrn_forget(to_heads(p['hg_ff'], HGRN_HEADS), lb_f)
    lf_b, k_b = hgrn_forget(to_heads(p['hg_fb'], HGRN_HEADS), lb_b)
    o_f, hs_f = gated_scan(q, k_f, iv, lf_f, hg_f0)
    o_b, hs_b = gated_scan(flip_t(q), flip_t(k_b), flip_t(iv), flip_t(lf_b), hg_b0)
    o_hg = rmsnorm(o_f + flip_t(o_b), lw['hg_onorm']) * jax.nn.silu(to_heads(p['hg_g'], HGRN_HEADS))

    q = rmsnorm(to_heads(p['nat_q'], NAT_HEADS), lw['nat_qnorm']) * (HEAD_DIM ** -0.5)
    k = rmsnorm(to_heads(p['nat_k'], NAT_HEADS), lw['nat_knorm'])
    v = to_heads(p['nat_v'], NAT_HEADS)
    if latent:
        o_nat = nat_latent(q, k, v, ctx_k, ctx_v, lw['nat_rpb'])
        new = None
    else:
        o_nat = dense_attn(q, k, v)
        new = (gs_f, gs_b, hs_f, hs_b, k, v)

    merged = jnp.concatenate([from_heads(o_gla), from_heads(o_hg), from_heads(o_nat)], axis=-1)
    return merged @ lw['w_out'], new


def expert_choice_moe(x, w_router, w_gate, w_up, w_down):
    b, t, dm = x.shape
    n = b * t
    xt = x.reshape(n, dm)
    cap = max(1, EC_CAPACITY * n // N_EXPERTS)
    aff = jax.nn.softmax((xt @ w_router).astype(jnp.float32), axis=-1)
    g, idx = lax.top_k(aff.T, cap)
    xe = xt[idx]
    hid = jax.nn.silu(jnp.einsum('ecd,edf->ecf', xe, w_gate)) * jnp.einsum('ecd,edf->ecf', xe, w_up)
    ye = jnp.einsum('ecf,efd->ecd', hid, w_down) * g[..., None].astype(x.dtype)
    out = jnp.zeros_like(xt).at[idx.reshape(-1)].add(ye.reshape(-1, dm))
    return out.reshape(b, t, dm)


def trunk_layer(x, cvec, lw, lb_f, lb_b, states, rope):
    m = jax.nn.silu(cvec) @ lw['w_mod'] + lw['b_mod']
    sh1, sc1, g1, sh2, sc2, g2 = jnp.split(m[:, None, :], N_MOD, axis=-1)
    h = rmsnorm(x, lw['norm1']) * (1 + sc1) + sh1
    mix, new = mixer(h, lw, lb_f, lb_b, states, rope)
    x = x + g1 * mix
    h = rmsnorm(x, lw['norm2']) * (1 + sc2) + sh2
    x = x + g2 * expert_choice_moe(h, lw['w_router'], lw['w_gate'], lw['w_up'], lw['w_down'])
    return x, new


def setup_inputs(seed: int = 0) -> dict:
    key = jax.random.key(seed)
    ks = jax.random.split(key, 32)
    f32 = jnp.float32

    def nrm(i, shape, s):
        return jax.random.normal(ks[i], shape, f32) * s

    return {
        'x_prompt': nrm(0, (BATCH, SEQ, D_MODEL), 1.0),
        'x_sample': nrm(1, (DEC_BATCH, DEC_SEQ, D_MODEL), 1.0),
        'state_gla': nrm(2, (DEC_BATCH, DEPTH, 2, GLA_HEADS, HEAD_DIM, HEAD_DIM), 1.0),
        'state_hgrn': nrm(3, (DEC_BATCH, DEPTH, 2, HGRN_HEADS, HEAD_DIM, HEAD_DIM), 1.0),
        'cache_nat_k': nrm(4, (DEC_BATCH, DEPTH, NAT_HEADS, PAST_LEN, HEAD_DIM), 1.0),
        'cache_nat_v': nrm(5, (DEC_BATCH, DEPTH, NAT_HEADS, PAST_LEN, HEAD_DIM), 1.0),
        'c': nrm(6, (DEC_BATCH, D_MODEL), 1.0),
        'c_ctx': nrm(7, (D_MODEL,), 1.0),
        'w_mod': nrm(8, (DEPTH, D_MODEL, N_MOD * D_MODEL), D_MODEL ** -0.5),
        'b_mod': nrm(9, (DEPTH, N_MOD * D_MODEL), 0.02),
        'norm1': 1.0 + nrm(10, (DEPTH, D_MODEL), 0.05),
        'norm2': 1.0 + nrm(11, (DEPTH, D_MODEL), 0.05),
        'w_in': nrm(12, (DEPTH, D_MODEL, P_IN), D_MODEL ** -0.5),
        'gla_wa_f': nrm(13, (DEPTH, GLA_LOWRANK, GLA_W), GLA_LOWRANK ** -0.5),
        'gla_wa_b': nrm(14, (DEPTH, GLA_LOWRANK, GLA_W), GLA_LOWRANK ** -0.5),
        'gla_ba_f': nrm(15, (DEPTH, GLA_W), 0.1),
        'gla_ba_b': nrm(16, (DEPTH, GLA_W), 0.1),
        'gla_onorm': 1.0 + nrm(17, (DEPTH, HEAD_DIM), 0.05),
        'hg_lb_logits': nrm(18, (2, DEPTH, HGRN_W), 0.5),
        'hg_onorm': 1.0 + nrm(19, (DEPTH, HEAD_DIM), 0.05),
        'nat_qnorm': 1.0 + nrm(20, (DEPTH, HEAD_DIM), 0.05),
        'nat_knorm': 1.0 + nrm(21, (DEPTH, HEAD_DIM), 0.05),
        'nat_rpb': nrm(22, (DEPTH, NAT_HEADS, 2 * NAT_KR - 1, 2 * NAT_KC - 1), 0.1),
        'w_out': nrm(23, (DEPTH, MIX_W, D_MODEL), MIX_W ** -0.5),
        'w_router': nrm(24, (DEPTH, D_MODEL, N_EXPERTS), D_MODEL ** -0.5),
        'w_gate': nrm(25, (DEPTH, N_EXPERTS, D_MODEL, EXPERT_FF), D_MODEL ** -0.5),
        'w_up': nrm(26, (DEPTH, N_EXPERTS, D_MODEL, EXPERT_FF), D_MODEL ** -0.5),
        'w_down': nrm(27, (DEPTH, N_EXPERTS, EXPERT_FF, D_MODEL), EXPERT_FF ** -0.5),
    }


def reference(x_prompt, x_sample, state_gla, state_hgrn, cache_nat_k, cache_nat_v, c, c_ctx,
              w_mod, b_mod, norm1, norm2, w_in, gla_wa_f, gla_wa_b, gla_ba_f, gla_ba_b, gla_onorm,
              hg_lb_logits, hg_onorm, nat_qnorm, nat_knorm, nat_rpb, w_out,
              w_router, w_gate, w_up, w_down):
    lb = jnp.cumsum(jax.nn.softmax(hg_lb_logits.astype(jnp.float32), axis=1), axis=1)
    lb = lb - lb[:, :1]

    def layer_weights(l):
        return dict(w_mod=w_mod[l], b_mod=b_mod[l], norm1=norm1[l], norm2=norm2[l], w_in=w_in[l],
                    gla_wa_f=gla_wa_f[l], gla_wa_b=gla_wa_b[l], gla_ba_f=gla_ba_f[l], gla_ba_b=gla_ba_b[l],
                    gla_onorm=gla_onorm[l], hg_onorm=hg_onorm[l], nat_qnorm=nat_qnorm[l],
                    nat_knorm=nat_knorm[l], nat_rpb=nat_rpb[l], w_out=w_out[l], w_router=w_router[l],
                    w_gate=w_gate[l], w_up=w_up[l], w_down=w_down[l])

    x = x_prompt
    gla_new, hg_new, k_new, v_new = [], [], [], []
    for l in range(DEPTH):
        x, st = trunk_layer(x, c_ctx[None, :], layer_weights(l), lb[0, l], lb[1, l], None, None)
        gs_f, gs_b, hs_f, hs_b, kk, vv = st
        gla_new.append(jnp.stack([gs_f, gs_b], axis=1))
        hg_new.append(jnp.stack([hs_f, hs_b], axis=1))
        k_new.append(kk)
        v_new.append(vv)
    y_prompt = x
    new_state_gla = jnp.stack(gla_new, axis=1)
    new_state_hgrn = jnp.stack(hg_new, axis=1)
    new_cache_nat_k = jnp.stack(k_new, axis=1)
    new_cache_nat_v = jnp.stack(v_new, axis=1)

    rope = axial_rope_tables(x_sample.shape[1], x_sample.dtype)
    x = x_sample
    for l in range(DEPTH):
        states = (state_gla[:, l, 0], state_gla[:, l, 1], state_hgrn[:, l, 0], state_hgrn[:, l, 1],
                  cache_nat_k[:, l], cache_nat_v[:, l])
        x, _ = trunk_layer(x, c, layer_weights(l), lb[0, l], lb[1, l], states, rope)
    y_sample = x
    return (y_prompt, y_sample, new_state_gla, new_state_hgrn, new_cache_nat_k, new_cache_nat_v)
```

```python
import functools
import math

import jax
import jax.numpy as jnp
from jax import lax
from jax.experimental import pallas as pl
from jax.experimental.pallas import tpu as pltpu

F32 = jnp.float32
BF16 = jnp.bfloat16

D_MODEL = 2048
BATCH = 16
SEQ = 256
DEPTH = 4
DEC_BATCH = 4
DEC_SEQ = 1024
PAST_LEN = 512
GRID_W = 64
HEAD_DIM = 128
GLA_HEADS = 4
HGRN_HEADS = 4
NAT_HEADS = 8
GLA_W = GLA_HEADS * HEAD_DIM
HGRN_W = HGRN_HEADS * HEAD_DIM
NAT_W = NAT_HEADS * HEAD_DIM
GLA_LOWRANK = 16
GLA_TAU = 16.0
NAT_KR = 8
NAT_KC = 16
ROPE_BASE = 10000.0
ROPE_AXIS = HEAD_DIM // 2
N_EXPERTS = 16
EXPERT_FF = D_MODEL // 2
EC_CAPACITY = 2
EPS = 1e-6
N_MOD = 6

N_CTX = BATCH * SEQ
N_LAT = DEC_BATCH * DEC_SEQ
N_TOK = N_CTX + N_LAT
GROUP_ROWS = 1024
N_GROUPS = N_TOK // GROUP_ROWS
CAP = EC_CAPACITY * N_CTX // N_EXPERTS
LANES = 128
NEG = -1e30

NQ, NK, NV = 0, NAT_W, 2 * NAT_W
GQ = 3 * NAT_W
GK, GV, GG = GQ + GLA_W, GQ + 2 * GLA_W, GQ + 3 * GLA_W
HQ = GQ + 4 * GLA_W
HI, HG, HFF, HFB = HQ + HGRN_W, HQ + 2 * HGRN_W, HQ + 3 * HGRN_W, HQ + 4 * HGRN_W
LR = HQ + 5 * HGRN_W
P_PAD = 8192
CHUNK = 128
VMEM_LIMIT = 56 * 1024 * 1024


def _cparams(sem, vmem=VMEM_LIMIT):
    return pltpu.CompilerParams(dimension_semantics=sem, vmem_limit_bytes=vmem)


def _sigmoid(x):
    return 1.0 / (1.0 + jnp.exp(-x))


def _silu(x):
    return x * _sigmoid(x)


def _log_sigmoid(x):
    return jnp.minimum(x, 0.0) - jnp.log1p(jnp.exp(-jnp.abs(x)))


def _dot(a, b):
    return jnp.dot(a, b, preferred_element_type=F32)


def _dot_nt(a, b):
    return lax.dot_general(a, b, (((1,), (1,)), ((), ())), preferred_element_type=F32)


def _dot_tn(a, b):
    return lax.dot_general(a, b, (((0,), (0,)), ((), ())), preferred_element_type=F32)


def _mod_kernel(c_ref, w_ref, b_ref, o_ref):
    s = _silu(c_ref[...]).astype(BF16)
    o_ref[...] = _dot(s, w_ref[...].astype(BF16)) + b_ref[...]


def modulation(cvec8, w_mod, b_mod):
    depth, d, n = w_mod.shape
    tn = 1024
    return pl.pallas_call(
        _mod_kernel,
        grid=(depth, n // tn),
        in_specs=[pl.BlockSpec((8, d), lambda l, j: (0, 0)),
                  pl.BlockSpec((None, d, tn), lambda l, j: (l, 0, j)),
                  pl.BlockSpec((None, 1, tn), lambda l, j: (l, 0, j))],
        out_specs=pl.BlockSpec((None, 8, tn), lambda l, j: (l, 0, j)),
        out_shape=jax.ShapeDtypeStruct((depth, 8, n), F32),
        compiler_params=_cparams(("arbitrary", "arbitrary")),
        name="modulation",
    )(cvec8, w_mod, b_mod.reshape(depth, 1, n))


def _norm_body(x_ref, w_ref, sc_ref, sh_ref):
    x = x_ref[...]
    y = x * lax.rsqrt(jnp.mean(x * x, axis=-1, keepdims=True) + EPS)
    return (y * w_ref[...]) * (1.0 + sc_ref[...]) + sh_ref[...]


def _norm_kernel(x_ref, w_ref, sc_ref, sh_ref, o_ref):
    o_ref[...] = _norm_body(x_ref, w_ref, sc_ref, sh_ref).astype(o_ref.dtype)


def _norm_router_kernel(x_ref, w_ref, sc_ref, sh_ref, wr_ref, o_ref, aff_ref):
    h = _norm_body(x_ref, w_ref, sc_ref, sh_ref)
    logits = jnp.dot(h, wr_ref[...], preferred_element_type=F32, precision=lax.Precision.HIGHEST)
    lane = lax.broadcasted_iota(jnp.int32, logits.shape, 1)
    logits = jnp.where(lane < N_EXPERTS, logits, NEG)
    e = jnp.exp(logits - jnp.max(logits, axis=-1, keepdims=True))
    aff = e / jnp.sum(e, axis=-1, keepdims=True)
    o_ref[:, :D_MODEL] = h
    o_ref[:, D_MODEL:] = aff
    aff_ref[...] = aff


def _mod_spec(which, tr):
    per = GROUP_ROWS // tr
    return pl.BlockSpec((None, None, 1, D_MODEL), lambda i: (i // per, which, 0, 0))


def norm_mod(x, w, modg, which_sc, which_sh, tr=256):
    n, d = x.shape
    return pl.pallas_call(
        _norm_kernel,
        grid=(n // tr,),
        in_specs=[pl.BlockSpec((tr, d), lambda i: (i, 0)),
                  pl.BlockSpec((1, d), lambda i: (0, 0)),
                  _mod_spec(which_sc, tr), _mod_spec(which_sh, tr)],
        out_specs=pl.BlockSpec((tr, d), lambda i: (i, 0)),
        out_shape=jax.ShapeDtypeStruct((n, d), BF16),
        compiler_params=_cparams(("arbitrary",)),
        name="norm_mod",
    )(x, w.reshape(1, d), modg, modg)


def norm_router(x, w, modg, which_sc, which_sh, wr_pad, tr=256):
    n, d = x.shape
    return pl.pallas_call(
        _norm_router_kernel,
        grid=(n // tr,),
        in_specs=[pl.BlockSpec((tr, d), lambda i: (i, 0)),
                  pl.BlockSpec((1, d), lambda i: (0, 0)),
                  _mod_spec(which_sc, tr), _mod_spec(which_sh, tr),
                  pl.BlockSpec((d, LANES), lambda i: (0, 0))],
        out_specs=[pl.BlockSpec((tr, d + LANES), lambda i: (i, 0)),
                   pl.BlockSpec((tr, LANES), lambda i: (i, 0))],
        out_shape=[jax.ShapeDtypeStruct((n, d + LANES), F32),
                   jax.ShapeDtypeStruct((n, LANES), F32)],
        compiler_params=_cparams(("arbitrary",)),
        name="norm_router",
    )(x, w.reshape(1, d), modg, modg, wr_pad)


def _mm_kernel(a_ref, w_ref, o_ref):
    o_ref[...] = _dot(a_ref[...], w_ref[...])


def in_proj(h, w_al, tm=1024, tn=512):
    m, k = h.shape
    n = w_al.shape[1]
    return pl.pallas_call(
        _mm_kernel,
        grid=(m // tm, n // tn),
        in_specs=[pl.BlockSpec((tm, k), lambda i, j: (i, 0)),
                  pl.BlockSpec((k, tn), lambda i, j: (0, j))],
        out_specs=pl.BlockSpec((tm, tn), lambda i, j: (i, j)),
        out_shape=jax.ShapeDtypeStruct((m, n), F32),
        compiler_params=_cparams(("arbitrary", "arbitrary")),
        name="in_proj",
    )(h, w_al)


def _outproj_kernel(a_nat, a_gla, a_hg, w_ref, x_ref, g_ref, o_ref):
    acc = _dot(a_gla[...], w_ref[0:GLA_W, :].astype(BF16))
    acc += _dot(a_hg[...], w_ref[GLA_W:GLA_W + HGRN_W, :].astype(BF16))
    acc += _dot(a_nat[...], w_ref[GLA_W + HGRN_W:, :].astype(BF16))
    o_ref[...] = x_ref[...] + g_ref[...] * acc


def out_proj(m_nat, m_gla, m_hg, w_out, x, modg, which_g, tm=1024, tn=512):
    n, d = x.shape
    return pl.pallas_call(
        _outproj_kernel,
        grid=(n // tm, d // tn),
        in_specs=[pl.BlockSpec((tm, NAT_W), lambda i, j: (i, 0)),
                  pl.BlockSpec((tm, GLA_W), lambda i, j: (i, 0)),
                  pl.BlockSpec((tm, HGRN_W), lambda i, j: (i, 0)),
                  pl.BlockSpec((d, tn), lambda i, j: (0, j)),
                  pl.BlockSpec((tm, tn), lambda i, j: (i, j)),
                  pl.BlockSpec((None, None, 1, tn), lambda i, j: (i, which_g, 0, j))],
        out_specs=pl.BlockSpec((tm, tn), lambda i, j: (i, j)),
        out_shape=jax.ShapeDtypeStruct((n, d), F32),
        compiler_params=_cparams(("arbitrary", "arbitrary")),
        name="out_proj",
    )(m_nat, m_gla, m_hg, w_out, x, modg)


def _chunk_intra(q, kf, kb, v, laf, lab):
    c = q.shape[0]
    row = lax.broadcasted_iota(jnp.int32, (c, c), 0)
    col = lax.broadcasted_iota(jnp.int32, (c, c), 1)
    xr = row ^ col
    d_f = jnp.where(row > col, xr, 0)
    d_b = jnp.where(row < col, xr, 0)
    rbit = lax.broadcasted_iota(jnp.int32, (c, HEAD_DIM), 0)

    qb = q.astype(BF16)
    eye = row == col
    scores = (jnp.where(eye, _dot_nt(qb, kf.astype(BF16)), 0.0)
              + jnp.where(eye, _dot_nt(qb, kb.astype(BF16)), 0.0))

    pf, tf = laf, laf
    pb, tb = lab, lab
    m, lg = 1, 0
    while m < c:
        bit = (rbit & m) != 0
        wf = jnp.exp(jnp.where(bit, pf, tf - pf))
        rf = _dot_nt((q * wf).astype(BF16), (kf * wf).astype(BF16))
        scores += jnp.where((d_f >> lg) == 1, rf, 0.0)
        wb = jnp.exp(jnp.where(bit, pb - lab, tb - pb + lab))
        rb = _dot_nt((q * wb).astype(BF16), (kb * wb).astype(BF16))
        scores += jnp.where((d_b >> lg) == 1, rb, 0.0)
        sf = jnp.where(bit, pltpu.roll(tf, m, 0), pltpu.roll(tf, c - m, 0))
        pf = pf + jnp.where(bit, sf, 0.0)
        tf = tf + sf
        sb = jnp.where(bit, pltpu.roll(tb, m, 0), pltpu.roll(tb, c - m, 0))
        pb = pb + jnp.where(bit, sb, 0.0)
        tb = tb + sb
        m, lg = 2 * m, lg + 1

    o_intra = _dot(scores.astype(BF16), v.astype(BF16))
    qdf = (q * jnp.exp(pf)).astype(BF16)
    kdf = (kf * jnp.exp(tf - pf)).astype(BF16)
    qdb = (q * jnp.exp(tb - pb + lab)).astype(BF16)
    kdb = (kb * jnp.exp(pb - lab)).astype(BF16)
    return o_intra, qdf, kdf, qdb, kdb, jnp.exp(tf[0:1, :]), jnp.exp(tb[0:1, :])


def _scan_kernel(*refs, kind, latent, t_len):
    n_chunks = t_len // CHUNK
    if kind == "gla":
        pq, pk, pv, pg, plr, waf, wab, prm = refs[:8]
        rest = refs[8:]
        if latent:
            cos, sin, s0f, s0b = rest[:4]
            rest = rest[4:]
    else:
        pq, pv, pg, pff, pfb, prm = refs[:6]
        rest = refs[6:]
        if latent:
            s0f, s0b = rest[:2]
            rest = rest[2:]
    o_ref, sf_ref, sb_ref = rest[:3]
    q_s, kf_s, kb_s, laf_s, lab_s, qdf_s, kdf_s, qdb_s, kdb_s, o_s, etf_s, etb_s, stf_s, stb_s = rest[3:]

    if kind == "gla":
        q = pq[...] * (HEAD_DIM ** -0.5)
        k = pk[...]
        if latent:
            lane = lax.broadcasted_iota(jnp.int32, q.shape, 1)
            first = (lane & (ROPE_AXIS // 2)) == 0

            def rope(x):
                half = ROPE_AXIS // 2
                swapped = jnp.where(first, pltpu.roll(x, HEAD_DIM - half, 1), pltpu.roll(x, half, 1))
                return x * cos[...] + swapped * sin[...]

            q = rope(q)
            k = rope(k)
        q_s[...] = q
        kf_s[...] = k
        lr = plr[...].astype(BF16)
        zf = _dot(lr, waf[...].astype(BF16)) + prm[0:1, :]
        zb = _dot(lr, wab[...].astype(BF16)) + prm[1:2, :]
        laf_s[...] = _log_sigmoid(zf) / GLA_TAU
        lab_s[...] = _log_sigmoid(zb) / GLA_TAU
        onorm = prm[2:3, :]
        kb_src = kf_s
    else:
        q_s[...] = _silu(pq[...])

        def forget(z, loglb, log1m, om):
            a = loglb
            b = log1m + _log_sigmoid(z)
            la = jnp.maximum(a, b) + jnp.log1p(jnp.exp(-jnp.abs(a - b)))
            return la, om * _sigmoid(-z)

        la, kk = forget(pff[...], prm[0:1, :], prm[1:2, :], prm[2:3, :])
        laf_s[...] = la
        kf_s[...] = kk
        la, kk = forget(pfb[...], prm[3:4, :], prm[4:5, :], prm[5:6, :])
        lab_s[...] = la
        kb_s[...] = kk
        onorm = prm[6:7, :]
        kb_src = kb_s

    def pass_a(n, carry):
        sl = pl.ds(pl.multiple_of(n * CHUNK, CHUNK), CHUNK)
        o_intra, qdf, kdf, qdb, kdb, etf, etb = _chunk_intra(
            q_s[sl, :], kf_s[sl, :], kb_src[sl, :], pv[sl, :], laf_s[sl, :], lab_s[sl, :])
        o_s[sl, :] = o_intra
        qdf_s[sl, :] = qdf
        kdf_s[sl, :] = kdf
        qdb_s[sl, :] = qdb
        kdb_s[sl, :] = kdb
        etf_s[pl.ds(n, 1), :] = etf
        etb_s[pl.ds(n, 1), :] = etb
        return carry

    lax.fori_loop(0, n_chunks, pass_a, 0)

    if latent:
        stf_s[...] = s0f[...].T
        stb_s[...] = s0b[...].T
    else:
        stf_s[...] = jnp.zeros((HEAD_DIM, HEAD_DIM), F32)
        stb_s[...] = jnp.zeros((HEAD_DIM, HEAD_DIM), F32)

    def pass_b(n, carry):
        sl = pl.ds(pl.multiple_of(n * CHUNK, CHUNK), CHUNK)
        nb = n_chunks - 1 - n
        slb = pl.ds(pl.multiple_of(nb * CHUNK, CHUNK), CHUNK)
        st = stf_s[...]
        o_s[sl, :] += _dot_nt(qdf_s[sl, :], st.astype(BF16))
        stf_s[...] = st * etf_s[pl.ds(n, 1), :] + _dot_tn(pv[sl, :].astype(BF16), kdf_s[sl, :])
        st = stb_s[...]
        o_s[slb, :] += _dot_nt(qdb_s[slb, :], st.astype(BF16))
        stb_s[...] = st * etb_s[pl.ds(nb, 1), :] + _dot_tn(pv[slb, :].astype(BF16), kdb_s[slb, :])
        return carry

    lax.fori_loop(0, n_chunks, pass_b, 0)

    sf_ref[...] = stf_s[...].T
    sb_ref[...] = stb_s[...].T

    o = o_s[...]
    y = o * lax.rsqrt(jnp.mean(o * o, axis=-1, keepdims=True) + EPS) * onorm
    o_ref[...] = (y * _silu(pg[...])).astype(o_ref.dtype)


def gated_scan_group(p, kind, latent, extra, s0=None):
    t_len = DEC_SEQ if latent else SEQ
    nb = DEC_BATCH if latent else BATCH
    n_heads = GLA_HEADS
    row0 = (N_CTX // t_len) if latent else 0

    def col(off):
        return pl.BlockSpec((t_len, HEAD_DIM), lambda b, h, off=off: (row0 + b, off // HEAD_DIM + h))

    def per_head(rows):
        return pl.BlockSpec((rows, HEAD_DIM), lambda b, h: (0, h))

    if kind == "gla":
        waf, wab, prm, cos, sin = extra
        in_specs = [col(GQ), col(GK), col(GV), col(GG),
                    pl.BlockSpec((t_len, HEAD_DIM), lambda b, h: (row0 + b, LR // HEAD_DIM)),
                    per_head(HEAD_DIM), per_head(HEAD_DIM), per_head(8)]
        args = [p, p, p, p, p, waf, wab, prm]
        if latent:
            in_specs += [pl.BlockSpec((t_len, HEAD_DIM), lambda b, h: (0, 0))] * 2
            args += [cos, sin]
    else:
        (prm,) = extra
        in_specs = [col(HQ), col(HI), col(HG), col(HFF), col(HFB), per_head(8)]
        args = [p, p, p, p, p, prm]
    if latent:
        st_spec = pl.BlockSpec((None, None, None, HEAD_DIM, HEAD_DIM), lambda b, h: (b, 0, h, 0, 0))
        st_spec_b = pl.BlockSpec((None, None, None, HEAD_DIM, HEAD_DIM), lambda b, h: (b, 1, h, 0, 0))
        in_specs += [st_spec, st_spec_b]
        args += [s0, s0]

    width = n_heads * HEAD_DIM
    st_out = pl.BlockSpec((None, None, HEAD_DIM, HEAD_DIM), lambda b, h: (b, h, 0, 0))
    out_specs = [pl.BlockSpec((t_len, HEAD_DIM), lambda b, h: (b, h)), st_out, st_out]
    out_shape = [jax.ShapeDtypeStruct((nb * t_len, width), BF16),
                 jax.ShapeDtypeStruct((nb, n_heads, HEAD_DIM, HEAD_DIM), F32),
                 jax.ShapeDtypeStruct((nb, n_heads, HEAD_DIM, HEAD_DIM), F32)]
    tok = lambda dt: pltpu.VMEM((t_len, HEAD_DIM), dt)
    n_chunks = t_len // CHUNK
    scratch = [tok(F32)] * 5 + [tok(BF16)] * 4 + [tok(F32)] + \
              [pltpu.VMEM((n_chunks, HEAD_DIM), F32)] * 2 + [pltpu.VMEM((HEAD_DIM, HEAD_DIM), F32)] * 2
    o, sf, sb = pl.pallas_call(
        functools.partial(_scan_kernel, kind=kind, latent=latent, t_len=t_len),
        grid=(nb, n_heads),
        in_specs=in_specs,
        out_specs=out_specs,
        out_shape=out_shape,
        scratch_shapes=scratch,
        compiler_params=_cparams(("arbitrary", "arbitrary")),
        name=f"scan_{kind}_{'lat' if latent else 'ctx'}",
    )(*args)
    return o, sf, sb


def _rms_rows(x, g):
    return x * lax.rsqrt(jnp.mean(x * x, axis=-1, keepdims=True) + EPS) * g


def _nat_ctx_kernel(pq, pk, pv, prm, o_ref, k_ref, v_ref):
    qn, kn = prm[0:1, :], prm[1:2, :]
    for h in range(NAT_HEADS):
        cs = slice(h * HEAD_DIM, (h + 1) * HEAD_DIM)
        q = _rms_rows(pq[:, cs], qn) * (HEAD_DIM ** -0.5)
        k = _rms_rows(pk[:, cs], kn)
        v = pv[:, cs]
        k_ref[h] = k
        v_ref[h] = v
        s = _dot_nt(q.astype(BF16), k.astype(BF16))
        e = jnp.exp(s - jnp.max(s, axis=-1, keepdims=True))
        o = _dot(e.astype(BF16), v.astype(BF16)) / jnp.sum(e, axis=-1, keepdims=True)
        o_ref[:, cs] = o.astype(o_ref.dtype)


def nat_ctx(p, prm):
    blk = lambda off: pl.BlockSpec((SEQ, NAT_W), lambda b, off=off: (b, off // NAT_W))
    cache = pl.BlockSpec((None, NAT_HEADS, SEQ, HEAD_DIM), lambda b: (b, 0, 0, 0))
    return pl.pallas_call(
        _nat_ctx_kernel,
        grid=(BATCH,),
        in_specs=[blk(NQ), blk(NK), blk(NV), pl.BlockSpec((8, HEAD_DIM), lambda b: (0, 0))],
        out_specs=[pl.BlockSpec((SEQ, NAT_W), lambda b: (b, 0)), cache, cache],
        out_shape=[jax.ShapeDtypeStruct((N_CTX, NAT_W), BF16),
                   jax.ShapeDtypeStruct((BATCH, NAT_HEADS, SEQ, HEAD_DIM), F32),
                   jax.ShapeDtypeStruct((BATCH, NAT_HEADS, SEQ, HEAD_DIM), F32)],
        compiler_params=_cparams(("arbitrary",)),
        name="nat_ctx",
    )(p, p, p, prm)


NAT_TQ = 256


def _nat_lat_kernel(pq, pk, pv, ck, cv, bias, prm, o_ref):
    qn, kn = prm[0:1, :], prm[1:2, :]
    k = _rms_rows(pk[...], kn).astype(BF16)
    v = pv[...].astype(BF16)
    ckb = ck[...].astype(BF16)
    cvb = cv[...].astype(BF16)
    for t in range(DEC_SEQ // NAT_TQ):
        rs = slice(t * NAT_TQ, (t + 1) * NAT_TQ)
        q = (_rms_rows(pq[rs, :], qn) * (HEAD_DIM ** -0.5)).astype(BF16)
        s1 = _dot_nt(q, k) + bias[rs, :]
        s2 = _dot_nt(q, ckb)
        mx = jnp.maximum(jnp.max(s1, axis=-1, keepdims=True), jnp.max(s2, axis=-1, keepdims=True))
        e1 = jnp.exp(s1 - mx)
        e2 = jnp.exp(s2 - mx)
        den = jnp.sum(e1, axis=-1, keepdims=True) + jnp.sum(e2, axis=-1, keepdims=True)
        o = (_dot(e1.astype(BF16), v) + _dot(e2.astype(BF16), cvb)) / den
        o_ref[rs, :] = o.astype(o_ref.dtype)


def nat_lat(p, cache_k, cache_v, layer, bias, prm):
    row0 = N_CTX // DEC_SEQ
    blk = lambda off: pl.BlockSpec((DEC_SEQ, HEAD_DIM), lambda h, b, off=off: (row0 + b, off // HEAD_DIM + h))
    cache = pl.BlockSpec((None, None, None, PAST_LEN, HEAD_DIM), lambda h, b: (b, layer, h, 0, 0))
    return pl.pallas_call(
        _nat_lat_kernel,
        grid=(NAT_HEADS, DEC_BATCH),
        in_specs=[blk(NQ), blk(NK), blk(NV), cache, cache,
                  pl.BlockSpec((None, DEC_SEQ, DEC_SEQ), lambda h, b: (h, 0, 0)),
                  pl.BlockSpec((8, HEAD_DIM), lambda h, b: (0, 0))],
        out_specs=pl.BlockSpec((DEC_SEQ, HEAD_DIM), lambda h, b: (b, h)),
        out_shape=jax.ShapeDtypeStruct((N_LAT, NAT_W), BF16),
        compiler_params=_cparams(("arbitrary", "arbitrary")),
        name="nat_lat",
    )(p, p, p, cache_k, cache_v, bias, prm)


def nat_bias(rpb):
    rows = DEC_SEQ // GRID_W
    kr = min(NAT_KR, rows)
    pos = jnp.arange(DEC_SEQ)
    r, c = pos // GRID_W, pos % GRID_W
    r_start = jnp.clip(r - kr // 2, 0, rows - kr)
    c_start = jnp.clip(c - NAT_KC // 2, 0, GRID_W - NAT_KC)
    row_ok = (r[None, :] >= r_start[:, None]) & (r[None, :] < r_start[:, None] + kr)
    col_ok = (c[None, :] >= c_start[:, None]) & (c[None, :] < c_start[:, None] + NAT_KC)
    row_idx = jnp.clip(r[None, :] - r[:, None] + NAT_KR - 1, 0, 2 * NAT_KR - 2)
    col_idx = jnp.clip(c[None, :] - c[:, None], -(NAT_KC - 1), NAT_KC - 1) + NAT_KC - 1
    b = rpb[:, row_idx, col_idx].astype(F32)
    return jnp.where((row_ok & col_ok)[None], b, NEG)


SEL_BLK = 128
N_SEL_BLK = N_CTX // SEL_BLK
COMB_R = 64
N_COMB = N_CTX // COMB_R


def _select_kernel(aff_ref, idx_ref, sel_ref, bend_ref, c_s, m_s):
    n = aff_ref.shape[0]
    bits = pltpu.bitcast(aff_ref[...], jnp.int32)

    def search(i, lo):
        cand = lo | (jnp.int32(1) << (30 - i))
        cnt = jnp.sum(jnp.where(bits >= cand, 1.0, 0.0), axis=0, keepdims=True)
        return jnp.where(cnt >= CAP, cand, lo)

    thr = lax.fori_loop(0, 31, search, jnp.zeros((1, LANES), jnp.int32))
    gt = bits > thr
    eq = bits == thr
    need = CAP - jnp.sum(jnp.where(gt, 1.0, 0.0), axis=0, keepdims=True)

    r_i = lax.broadcasted_iota(jnp.int32, (SEL_BLK, SEL_BLK), 0)
    c_i = lax.broadcasted_iota(jnp.int32, (SEL_BLK, SEL_BLK), 1)
    ltri = jnp.where(r_i >= c_i, 1.0, 0.0).astype(BF16)

    m_s[...] = jnp.where(eq, 1.0, 0.0)

    def prefix_pass(write_sel):
        off = jnp.zeros((1, LANES), F32)
        for b in range(N_SEL_BLK):
            rs = slice(b * SEL_BLK, (b + 1) * SEL_BLK)
            inc = _dot(ltri, m_s[rs, :].astype(BF16)) + off
            off = inc[SEL_BLK - 1:SEL_BLK, :]
            c_s[rs, :] = inc

    prefix_pass(False)
    sel = gt | (eq & (c_s[...] <= need))
    self32 = jnp.where(sel, 1.0, 0.0)
    sel_ref[...] = self32
    m_s[...] = self32
    prefix_pass(True)

    bend_ref[...] = c_s[pl.ds(COMB_R - 1, N_COMB, stride=COMB_R), :].astype(jnp.int32)

    jrow = lax.broadcasted_iota(jnp.int32, (1, CAP), 1).astype(F32)

    def count_blk(b, acc):
        cb = c_s[pl.ds(pl.multiple_of(b * SEL_BLK, SEL_BLK), SEL_BLK), :]
        rows = []
        for e in range(N_EXPERTS):
            col = cb[:, e:e + 1]
            rows.append(jnp.sum(jnp.where(col <= jrow, 1.0, 0.0), axis=0, keepdims=True))
        return acc + jnp.concatenate(rows, axis=0)

    acc = lax.fori_loop(0, n // SEL_BLK, count_blk, jnp.zeros((N_EXPERTS, CAP), F32))
    idx_ref[...] = acc.astype(jnp.int32)


def moe_select(aff):
    n_groups = N_TOK // N_CTX
    return pl.pallas_call(
        _select_kernel,
        grid=(n_groups,),
        in_specs=[pl.BlockSpec((N_CTX, LANES), lambda g: (g, 0))],
        out_specs=[pl.BlockSpec((None, N_EXPERTS, CAP), lambda g: (g, 0, 0)),
                   pl.BlockSpec((N_CTX, LANES), lambda g: (g, 0)),
                   pl.BlockSpec((None, N_COMB, LANES), lambda g: (g, 0, 0))],
        out_shape=[jax.ShapeDtypeStruct((n_groups, N_EXPERTS, CAP), jnp.int32),
                   jax.ShapeDtypeStruct((N_TOK, LANES), F32),
                   jax.ShapeDtypeStruct((n_groups, N_COMB, LANES), jnp.int32)],
        scratch_shapes=[pltpu.VMEM((N_CTX, LANES), F32), pltpu.VMEM((N_CTX, LANES), F32)],
        compiler_params=_cparams(("arbitrary",)),
        name="moe_select",
    )(aff)


FF_BLK = 256
N_FF = EXPERT_FF // FF_BLK
X_EXT = D_MODEL + LANES
ROWS_PER_EXPERT = 2 * CAP


def _ffn_kernel(idx_ref, hext, wg_ref, wu_ref, wd_ref, ye, xg, xb, acc, gsem, osem):
    e = pl.program_id(0)
    f = pl.program_id(1)

    def row_copy(ee, j, slot):
        g = j // CAP
        tok = idx_ref[g * N_EXPERTS + ee, j - g * CAP] + g * N_CTX
        return pltpu.make_async_copy(hext.at[pl.ds(tok, 1), :], xg.at[slot, pl.ds(j, 1), :], gsem.at[slot])

    def issue(ee, slot):
        def body(j, c):
            row_copy(ee, j, slot).start()
            return c
        lax.fori_loop(0, ROWS_PER_EXPERT, body, 0)

    def out_copy():
        return pltpu.make_async_copy(acc, ye.at[pl.ds(pl.multiple_of(e * ROWS_PER_EXPERT, ROWS_PER_EXPERT),
                                                      ROWS_PER_EXPERT), :], osem.at[0])

    slot = e % 2

    @pl.when(f == 0)
    def _():
        @pl.when(e == 0)
        def _():
            issue(0, 0)

        def wait_body(j, c):
            row_copy(e, j, slot).wait()
            return c
        lax.fori_loop(0, ROWS_PER_EXPERT, wait_body, 0)

        @pl.when(e + 1 < N_EXPERTS)
        def _():
            issue(e + 1, 1 - slot)

        xb[...] = xg[slot, :, :D_MODEL].astype(BF16)

    x = xb[...]
    gate = _dot(x, wg_ref[...].astype(BF16))
    up = _dot(x, wu_ref[...].astype(BF16))
    hid = (_silu(gate) * up).astype(BF16)
    part = _dot(hid, wd_ref[...].astype(BF16))

    @pl.when(f == 0)
    def _():
        acc[...] = part

    @pl.when(f > 0)
    def _():
        acc[...] += part

    @pl.when(f == N_FF - 1)
    def _():
        a = xg[slot, :, D_MODEL:]
        lane = lax.broadcasted_iota(jnp.int32, a.shape, 1)
        g = jnp.sum(jnp.where(lane == e, a, 0.0), axis=-1, keepdims=True)
        acc[...] = acc[...] * g
        cp = out_copy()
        cp.start()
        cp.wait()


def moe_ffn(idx_flat, hext, w_gate, w_up, w_down, layer):
    grid_spec = pltpu.PrefetchScalarGridSpec(
        num_scalar_prefetch=1,
        grid=(N_EXPERTS, N_FF),
        in_specs=[pl.BlockSpec(memory_space=pl.ANY),
                  pl.BlockSpec((None, None, D_MODEL, FF_BLK), lambda e, f, idx: (layer, e, 0, f)),
                  pl.BlockSpec((None, None, D_MODEL, FF_BLK), lambda e, f, idx: (layer, e, 0, f)),
                  pl.BlockSpec((None, None, FF_BLK, D_MODEL), lambda e, f, idx: (layer, e, f, 0))],
        out_specs=pl.BlockSpec(memory_space=pl.ANY),
        scratch_shapes=[pltpu.VMEM((2, ROWS_PER_EXPERT, X_EXT), F32),
                        pltpu.VMEM((ROWS_PER_EXPERT, D_MODEL), BF16),
                        pltpu.VMEM((ROWS_PER_EXPERT, D_MODEL), F32),
                        pltpu.SemaphoreType.DMA((2,)),
                        pltpu.SemaphoreType.DMA((1,))])
    return pl.pallas_call(
        _ffn_kernel,
        grid_spec=grid_spec,
        out_shape=jax.ShapeDtypeStruct((N_EXPERTS * ROWS_PER_EXPERT, D_MODEL), F32),
        compiler_params=_cparams(("arbitrary", "arbitrary")),
        name="moe_ffn",
    )(idx_flat, hext, w_gate, w_up, w_down)


def _combine_kernel(idx_ref, bend_ref, ye, sel_ref, x_ref, g_ref, o_ref, stage, sem):
    i = pl.program_id(0)
    n_tiles = pl.num_programs(0)

    def tile_loop(tile, slot, fn):
        g = tile // N_COMB
        tl = tile - g * N_COMB
        t0 = tl * COMB_R
        for e in range(N_EXPERTS):
            lo = jnp.where(tl == 0, 0, bend_ref[g * N_COMB + jnp.maximum(tl - 1, 0), e])
            hi = bend_ref[g * N_COMB + tl, e]

            def body(j, c, e=e):
                tok = idx_ref[g * N_EXPERTS + e, j]
                src = e * ROWS_PER_EXPERT + g * CAP + j
                fn(pltpu.make_async_copy(ye.at[pl.ds(src, 1), :],
                                         stage.at[slot, e, pl.ds(tok - t0, 1), :], sem.at[slot]))
                return c
            lax.fori_loop(lo, hi, body, 0)

    slot = i % 2

    @pl.when(i == 0)
    def _():
        tile_loop(i, slot, lambda cp: cp.start())

    @pl.when(i + 1 < n_tiles)
    def _():
        tile_loop(i + 1, 1 - slot, lambda cp: cp.start())

    tile_loop(i, slot, lambda cp: cp.wait())

    sel = sel_ref[...]
    tot = jnp.zeros((COMB_R, D_MODEL), F32)
    for e in range(N_EXPERTS):
        tot += jnp.where(sel[:, e:e + 1] > 0.5, stage[slot, e], 0.0)
    o_ref[...] = x_ref[...] + g_ref[...] * tot


def moe_combine(idx_flat, bend_flat, ye, sel, x, modg, which_g):
    per = GROUP_ROWS // COMB_R
    grid_spec = pltpu.PrefetchScalarGridSpec(
        num_scalar_prefetch=2,
        grid=(N_TOK // COMB_R,),
        in_specs=[pl.BlockSpec(memory_space=pl.ANY),
                  pl.BlockSpec((COMB_R, LANES), lambda i, a, b: (i, 0)),
                  pl.BlockSpec((COMB_R, D_MODEL), lambda i, a, b: (i, 0)),
                  pl.BlockSpec((None, None, 1, D_MODEL), lambda i, a, b: (i // per, which_g, 0, 0))],
        out_specs=pl.BlockSpec((COMB_R, D_MODEL), lambda i, a, b: (i, 0)),
        scratch_shapes=[pltpu.VMEM((2, N_EXPERTS, COMB_R, D_MODEL), F32),
                        pltpu.SemaphoreType.DMA((2,))])
    return pl.pallas_call(
        _combine_kernel,
        grid_spec=grid_spec,
        out_shape=jax.ShapeDtypeStruct((N_TOK, D_MODEL), F32),
        compiler_params=_cparams(("arbitrary",)),
        name="moe_combine",
    )(idx_flat, bend_flat, ye, sel, x, modg)


def _align_w_in(w):
    gla_end = 4 * GLA_W
    lr_end = gla_end + 2 * GLA_LOWRANK
    hg_end = lr_end + 5 * HGRN_W
    pad = jnp.zeros((D_MODEL, P_PAD - LR - 2 * GLA_LOWRANK), w.dtype)
    return jnp.concatenate([w[:, hg_end:], w[:, :gla_end], w[:, lr_end:hg_end], w[:, gla_end:lr_end], pad],
                           axis=1).astype(BF16)


def _rope_tables(t):
    pos = jnp.arange(t)
    rows = (pos // GRID_W).astype(F32)
    cols = (pos % GRID_W).astype(F32)
    inv = ROPE_BASE ** (-jnp.arange(0, ROPE_AXIS, 2, dtype=F32) / ROPE_AXIS)
    ar = rows[:, None] * inv[None, :]
    ac = cols[:, None] * inv[None, :]
    cos = jnp.concatenate([jnp.cos(ar), jnp.cos(ar), jnp.cos(ac), jnp.cos(ac)], axis=1)
    sin = jnp.concatenate([-jnp.sin(ar), jnp.sin(ar), -jnp.sin(ac), jnp.sin(ac)], axis=1)
    return cos, sin


def _pad_rows(rows):
    out = jnp.stack(rows, axis=0)
    return jnp.concatenate([out, jnp.zeros((8 - out.shape[0],) + out.shape[1:], out.dtype)], axis=0)


def kernel(x_prompt, x_sample, state_gla, state_hgrn, cache_nat_k, cache_nat_v, c, c_ctx, w_mod, b_mod, norm1, norm2, w_in, gla_wa_f, gla_wa_b, gla_ba_f, gla_ba_b, gla_onorm, hg_lb_logits, hg_onorm, nat_qnorm, nat_knorm, nat_rpb, w_out, w_router, w_gate, w_up, w_down):
    lb = jnp.cumsum(jax.nn.softmax(hg_lb_logits.astype(F32), axis=1), axis=1)
    lb = lb - lb[:, :1]

    cvec8 = jnp.concatenate([c_ctx[None, :], c, jnp.zeros((8 - 1 - DEC_BATCH, D_MODEL), F32)], axis=0)
    mod = modulation(cvec8, w_mod, b_mod)
    group_row = jnp.array([0] * (N_CTX // GROUP_ROWS) + list(range(1, DEC_BATCH + 1)), jnp.int32)
    modg_all = mod[:, group_row].reshape(DEPTH, N_GROUPS, N_MOD, 1, D_MODEL)

    cos, sin = _rope_tables(DEC_SEQ)
    x = jnp.concatenate([x_prompt.reshape(N_CTX, D_MODEL), x_sample.reshape(N_LAT, D_MODEL)], axis=0)

    gla_states, hg_states, k_caches, v_caches = [], [], [], []
    for l in range(DEPTH):
        modg = modg_all[l]
        h = norm_mod(x, norm1[l], modg, 1, 0)
        p = in_proj(h, _align_w_in(w_in[l]))

        zrows = jnp.zeros((HEAD_DIM - GLA_LOWRANK, GLA_W), F32)
        waf = jnp.concatenate([gla_wa_f[l], zrows], axis=0)
        wab = jnp.concatenate([jnp.zeros((GLA_LOWRANK, GLA_W), F32), gla_wa_b[l],
                               jnp.zeros((HEAD_DIM - 2 * GLA_LOWRANK, GLA_W), F32)], axis=0)
        gla_prm = _pad_rows([gla_ba_f[l], gla_ba_b[l], jnp.tile(gla_onorm[l], GLA_HEADS)])
        hg_prm = _pad_rows([jnp.log(lb[0, l]), jnp.log1p(-lb[0, l]), 1.0 - lb[0, l],
                            jnp.log(lb[1, l]), jnp.log1p(-lb[1, l]), 1.0 - lb[1, l],
                            jnp.tile(hg_onorm[l], HGRN_HEADS)])
        nat_prm = _pad_rows([nat_qnorm[l], nat_knorm[l]])

        g_ctx, gs_f, gs_b = gated_scan_group(p, "gla", False, (waf, wab, gla_prm, cos, sin))
        g_lat, _, _ = gated_scan_group(p, "gla", True, (waf, wab, gla_prm, cos, sin), state_gla[:, l])
        h_ctx, hs_f, hs_b = gated_scan_group(p, "hg", False, (hg_prm,))
        h_lat, _, _ = gated_scan_group(p, "hg", True, (hg_prm,), state_hgrn[:, l])
        n_ctx, k_new, v_new = nat_ctx(p, nat_prm)
        n_lat = nat_lat(p, cache_nat_k, cache_nat_v, l, nat_bias(nat_rpb[l]), nat_prm)

        gla_states.append(jnp.stack([gs_f, gs_b], axis=1))
        hg_states.append(jnp.stack([hs_f, hs_b], axis=1))
        k_caches.append(k_new)
        v_caches.append(v_new)

        x = out_proj(jnp.concatenate([n_ctx, n_lat], axis=0), jnp.concatenate([g_ctx, g_lat], axis=0),
                     jnp.concatenate([h_ctx, h_lat], axis=0), w_out[l], x, modg, 2)

        wr_pad = jnp.concatenate([w_router[l], jnp.zeros((D_MODEL, LANES - N_EXPERTS), F32)], axis=1)
        hext, aff = norm_router(x, norm2[l], modg, 4, 3, wr_pad)
        idx, sel, bend = moe_select(aff)
        idx_flat = idx.reshape(-1, CAP)
        ye = moe_ffn(idx_flat, hext, w_gate, w_up, w_down, l)
        x = moe_combine(idx_flat, bend.reshape(-1, LANES), ye, sel, x, modg, 5)

    y_prompt = x[:N_CTX].reshape(BATCH, SEQ, D_MODEL)
    y_sample = x[N_CTX:].reshape(DEC_BATCH, DEC_SEQ, D_MODEL)
    return (y_prompt, y_sample, jnp.stack(gla_states, axis=1), jnp.stack(hg_states, axis=1),
            jnp.stack(k_caches, axis=1), jnp.stack(v_caches, axis=1))
```
